```python
import jax, jax.numpy as jnp
from jax import lax
import numpy as np

D_MODEL = 2048
BATCH = 1
SEQ = 16384
DEPTH = 4
DEC_BATCH = 16
DEC_SEQ = 32
PAST_LEN = 4096

CHUNK = 64
D_MIX = D_MODEL
GROUP = D_MODEL // 16
N_POOL_G = 4
POOL_WINDOWS = (2, 4, 8, 16)
POOL_MAX = 16
D_POOL = N_POOL_G * GROUP
N_SGU_H = 6
D_SGU = N_SGU_H * GROUP
SGU_CHUNK = 128
N_CONV_G = 6
D_CONV = N_CONV_G * GROUP
CONV_K = 31
D_IN = D_POOL + 2 * D_SGU + 2 * D_CONV
SPLITS = (D_POOL, D_POOL + D_SGU, D_POOL + 2 * D_SGU, D_POOL + 2 * D_SGU + D_CONV)
D_FF = ((8 * D_MODEL // 3 + 255) // 256) * 256
EPS = 1e-6

kernel_name = 'hybrid_pool_sgu_conformer_stream_step'


def rms_norm(x, g):
    xf = x.astype(jnp.float32)
    y = xf * lax.rsqrt(jnp.mean(xf * xf, axis=-1, keepdims=True) + EPS)
    return (y * g.astype(jnp.float32)).astype(x.dtype)


def layer_norm(x, g, b):
    xf = x.astype(jnp.float32)
    mu = jnp.mean(xf, axis=-1, keepdims=True)
    xc = xf - mu
    y = xc * lax.rsqrt(jnp.mean(xc * xc, axis=-1, keepdims=True) + EPS)
    return (y * g.astype(jnp.float32) + b.astype(jnp.float32)).astype(x.dtype)


def swiglu(x, wg, wu, wd):
    return (jax.nn.silu(x @ wg) * (x @ wu)) @ wd


def pool_mixer(p, hist, pos0, pool_w, pool_scale):
    B, T, _ = p.shape
    H = hist.shape[1]
    seq = jnp.concatenate([hist, p], axis=1)
    padded = jnp.pad(seq.astype(jnp.float32), ((0, 0), (POOL_MAX, 0), (0, 0)))
    cs = jnp.cumsum(padded, axis=1)
    end = POOL_MAX + H
    cs_now = cs[:, end:end + T]
    pos = pos0 + jnp.arange(T)
    parts = []
    for g, w in enumerate(POOL_WINDOWS):
        sl = slice(g * GROUP, (g + 1) * GROUP)
        s = cs_now[..., sl] - cs[:, end - w:end - w + T, sl]
        cnt = jnp.minimum(w, pos + 1).astype(jnp.float32)[None, :, None]
        parts.append(s / cnt)
    pooled = jnp.concatenate(parts, axis=-1) - p.astype(jnp.float32)
    pooled = pooled.astype(p.dtype).reshape(B, T, N_POOL_G, GROUP)
    out = jnp.einsum('btgc,gcd->btgd', pooled, pool_w).reshape(B, T, D_POOL)
    return out * pool_scale, seq[:, -(POOL_MAX - 1):]


def sgu_mixer(u, v, sgu_w, sgu_b):
    B, T, _ = u.shape
    L = min(T, SGU_CHUNK)
    mask = jnp.tril(jnp.ones((L, L), dtype=bool))
    w = jnp.where(mask, sgu_w[:, :L, :L], 0)
    vb = v.reshape(B, T // L, L, N_SGU_H, GROUP)
    s = jnp.einsum('hqk,bnkhc->bnqhc', w, vb) + sgu_b[:, :L].T[None, None, :, :, None]
    return u * s.reshape(B, T, D_SGU)


def conv_mixer(za, zg, hist, conv_w, conv_b, ln_g, ln_b):
    c = za * jax.nn.sigmoid(zg)
    seq = jnp.concatenate([hist, c], axis=1)
    y = lax.conv_general_dilated(seq, conv_w[:, None, :], window_strides=(1,), padding='VALID',
                                 dimension_numbers=('NWC', 'WIO', 'NWC'),
                                 feature_group_count=D_CONV) + conv_b
    y = jax.nn.silu(layer_norm(y, ln_g, ln_b))
    return y, seq[:, -(CONV_K - 1):]


def layer(x, pool_hist, conv_hist, pos0, ng, wg, wu, wd, w_in_l, pool_w_l, pool_scale_l,
          sgu_g_l, sgu_w_l, sgu_b_l, conv_w_l, conv_b_l, ln_g_l, ln_b_l, w_out_l):
    x = x + 0.5 * rms_norm(swiglu(rms_norm(x, ng[0]), wg[0], wu[0], wd[0]), ng[1])
    h = rms_norm(x, ng[2])
    z = h @ w_in_l
    p, zu, zv, za, zg = jnp.split(z, SPLITS, axis=-1)
    y_pool, new_pool = pool_mixer(p, pool_hist, pos0, pool_w_l, pool_scale_l)
    u = jax.nn.gelu(zu, approximate=False)
    v = rms_norm(jax.nn.gelu(zv, approximate=False), sgu_g_l)
    y_sgu = sgu_mixer(u, v, sgu_w_l, sgu_b_l)
    y_conv, new_conv = conv_mixer(za, zg, conv_hist, conv_w_l, conv_b_l, ln_g_l, ln_b_l)
    m = jnp.concatenate([y_pool, y_sgu, y_conv], axis=-1) @ w_out_l
    x = x + rms_norm(m, ng[3])
    x = x + 0.5 * rms_norm(swiglu(rms_norm(x, ng[4]), wg[1], wu[1], wd[1]), ng[5])
    return x, new_pool, new_conv, v


def setup_inputs(seed: int = 0) -> dict:
    key = jax.random.key(seed)
    ks = jax.random.split(key, 19)

    def nrm(k, shape, s):
        return jax.random.normal(k, shape, jnp.float32) * s

    return {
        'x_prompt': nrm(ks[0], (BATCH, SEQ, D_MODEL), 1.0),
        'x_sample': nrm(ks[1], (DEC_BATCH, DEC_SEQ, D_MODEL), 1.0),
        'state_pool': nrm(ks[2], (DEPTH, DEC_BATCH, POOL_MAX - 1, D_POOL), 1.0),
        'state_conv': nrm(ks[3], (DEPTH, DEC_BATCH, CONV_K - 1, D_CONV), 0.5),
        'norm_g': 1.0 + nrm(ks[4], (DEPTH, 6, D_MODEL), 0.05),
        'ffn_w_gate': nrm(ks[5], (DEPTH, 2, D_MODEL, D_FF), D_MODEL ** -0.5),
        'ffn_w_up': nrm(ks[6], (DEPTH, 2, D_MODEL, D_FF), D_MODEL ** -0.5),
        'ffn_w_down': nrm(ks[7], (DEPTH, 2, D_FF, D_MODEL), D_FF ** -0.5),
        'w_in': nrm(ks[8], (DEPTH, D_MODEL, D_IN), D_MODEL ** -0.5),
        'pool_w': nrm(ks[9], (DEPTH, N_POOL_G, GROUP, GROUP), GROUP ** -0.5),
        'pool_scale': 1.0 + nrm(ks[10], (DEPTH, D_POOL), 0.1),
        'sgu_norm_g': 1.0 + nrm(ks[11], (DEPTH, D_SGU), 0.05),
        'sgu_w': nrm(ks[12], (DEPTH, N_SGU_H, SGU_CHUNK, SGU_CHUNK), SGU_CHUNK ** -0.5),
        'sgu_b': 1.0 + nrm(ks[13], (DEPTH, N_SGU_H, SGU_CHUNK), 0.1),
        'conv_w': nrm(ks[14], (DEPTH, CONV_K, D_CONV), CONV_K ** -0.5),
        'conv_b': nrm(ks[15], (DEPTH, D_CONV), 0.02),
        'conv_ln_g': 1.0 + nrm(ks[16], (DEPTH, D_CONV), 0.05),
        'conv_ln_b': nrm(ks[17], (DEPTH, D_CONV), 0.02),
        'w_out': nrm(ks[18], (DEPTH, D_MIX, D_MODEL), D_MIX ** -0.5),
    }


def reference(x_prompt, x_sample, state_pool, state_conv, norm_g, ffn_w_gate, ffn_w_up, ffn_w_down,
              w_in, pool_w, pool_scale, sgu_norm_g, sgu_w, sgu_b, conv_w, conv_b, conv_ln_g, conv_ln_b,
              w_out):
    yp = x_prompt
    ys = x_sample
    bp = x_prompt.shape[0]
    pool_p, conv_p, pool_s, conv_s, v_s = [], [], [], [], []
    for l in range(DEPTH):
        lw = (norm_g[l], ffn_w_gate[l], ffn_w_up[l], ffn_w_down[l], w_in[l], pool_w[l], pool_scale[l],
              sgu_norm_g[l], sgu_w[l], sgu_b[l], conv_w[l], conv_b[l], conv_ln_g[l], conv_ln_b[l], w_out[l])
        yp, pp, cp, _ = layer(yp, jnp.zeros((bp, 0, D_POOL), yp.dtype),
                              jnp.zeros((bp, CONV_K - 1, D_CONV), yp.dtype), 0, *lw)
        ys, ps, cs, vs = layer(ys, state_pool[l], state_conv[l], PAST_LEN, *lw)
        pool_p.append(pp)
        conv_p.append(cp)
        pool_s.append(ps)
        conv_s.append(cs)
        v_s.append(vs)
    new_pool_prompt = jnp.stack(pool_p)
    new_conv_prompt = jnp.stack(conv_p)
    new_pool_sample = jnp.stack(pool_s)
    new_conv_sample = jnp.stack(conv_s)
    new_sgu_v_sample = jnp.stack(v_s)
    return (yp, ys, new_pool_prompt, new_conv_prompt, new_pool_sample, new_conv_sample, new_sgu_v_sample)
```

```python
import functools

import jax
import jax.numpy as jnp
from jax import lax
from jax.experimental import pallas as pl
from jax.experimental.pallas import tpu as pltpu

D_MODEL = 2048
SEQ = 16384
DEPTH = 4
DEC_BATCH = 16
DEC_SEQ = 32
PAST_LEN = 4096
GROUP = 128
POOL_WINDOWS = (2, 4, 8, 16)
POOL_MAX = 16
D_POOL = 512
N_SGU_H = 6
D_SGU = 768
SGU_CHUNK = 128
D_CONV = 768
CONV_K = 31
D_IN = 3584
D_FF = 5632
EPS = 1e-6

N_TOK = SEQ + DEC_BATCH * DEC_SEQ
TM = 512
N_TILES = N_TOK // TM
TF = 512
CH = SGU_CHUNK
POOL_HIST = 16
CONV_HIST = 32
VMEM_LIMIT = 56 * 1024 * 1024

_BF16 = jnp.bfloat16
_F32 = jnp.float32


def _rms(x, g):
    return x * lax.rsqrt(jnp.mean(x * x, axis=-1, keepdims=True) + EPS) * g


def _gelu(x):
    return 0.5 * x * (1.0 + lax.erf(x * (0.5 ** 0.5)))


def _ffn_kernel(x_ref, gpre_ref, gpost_ref, wg_ref, wu_ref, wd_ref, o_ref, h_ref):
    j = pl.program_id(1)

    @pl.when(j == 0)
    def _():
        h_ref[...] = _rms(x_ref[...], gpre_ref[...]).astype(_BF16)

    h = h_ref[...]
    g = jnp.dot(h, wg_ref[...], preferred_element_type=_F32)
    u = jnp.dot(h, wu_ref[...], preferred_element_type=_F32)
    a = (g * jax.nn.sigmoid(g) * u).astype(_BF16)
    d = jnp.dot(a, wd_ref[...], preferred_element_type=_F32)

    @pl.when(j == 0)
    def _():
        o_ref[...] = d

    @pl.when(j > 0)
    def _():
        o_ref[...] += d

    @pl.when(j == pl.num_programs(1) - 1)
    def _():
        o_ref[...] = x_ref[...] + 0.5 * _rms(o_ref[...], gpost_ref[...])


def _ffn_call(x, norm_g4, wg, wu, wd, layer, f):
    gi_pre, gi_post = (0, 1) if f == 0 else (4, 5)
    return pl.pallas_call(
        _ffn_kernel,
        grid=(N_TILES, D_FF // TF),
        in_specs=[
            pl.BlockSpec((TM, D_MODEL), lambda i, j: (i, 0)),
            pl.BlockSpec((None, None, 1, D_MODEL), lambda i, j: (layer, gi_pre, 0, 0)),
            pl.BlockSpec((None, None, 1, D_MODEL), lambda i, j: (layer, gi_post, 0, 0)),
            pl.BlockSpec((None, None, D_MODEL, TF), lambda i, j: (layer, f, 0, j)),
            pl.BlockSpec((None, None, D_MODEL, TF), lambda i, j: (layer, f, 0, j)),
            pl.BlockSpec((None, None, TF, D_MODEL), lambda i, j: (layer, f, j, 0)),
        ],
        out_specs=pl.BlockSpec((TM, D_MODEL), lambda i, j: (i, 0)),
        out_shape=jax.ShapeDtypeStruct((N_TOK, D_MODEL), _F32),
        scratch_shapes=[pltpu.VMEM((TM, D_MODEL), _BF16)],
        compiler_params=pltpu.CompilerParams(
            dimension_semantics=("arbitrary", "arbitrary"), vmem_limit_bytes=VMEM_LIMIT),
        name=f"ffn_l{layer}_{f}",
    )(x, norm_g4, norm_g4, wg, wu, wd)


def _mixer_chunk(cc, tile, seg, refs):
    (x_ref, spool_ref, sconv_ref, g2_ref, g3_ref, win_ref, poolw_ref, pscale_ref, sgug_ref, sguw_ref,
     sgub_ref, convw_ref, convb_ref, lng_ref, lnb_ref, wout_ref,
     o_ref, pout_ref, cout_ref, vout_ref, pbuf, cbuf, ybuf, ycat) = refs
    sample = seg != CH
    nseg = CH // seg
    pstride = POOL_HIST + seg
    cstride = CONV_HIST + seg
    r0 = pl.multiple_of(cc * CH, CH)

    x = x_ref[pl.ds(r0, CH), :]
    h = _rms(x, g2_ref[...]).astype(_BF16)

    p = jnp.dot(h, win_ref[:, 0:D_POOL], preferred_element_type=_F32)
    pout_ref[pl.ds(r0, CH), :] = p
    for s in range(nseg):
        if sample:
            pbuf[s * pstride:s * pstride + POOL_HIST, :] = spool_ref[cc * nseg + s]
        pbuf[s * pstride + POOL_HIST:(s + 1) * pstride, :] = p[s * seg:(s + 1) * seg, :]
    t_in_seg = lax.broadcasted_iota(jnp.int32, (seg, GROUP), 0)
    pos = t_in_seg + (PAST_LEN if sample else tile * TM + r0)
    for g, w in enumerate(POOL_WINDOWS):
        cols = slice(g * GROUP, (g + 1) * GROUP)
        cnt = jnp.minimum(w, pos + 1).astype(_F32)
        parts = []
        for s in range(nseg):
            b0 = s * pstride + POOL_HIST
            cur = pbuf[b0:b0 + seg, cols]
            acc = cur
            for k in range(1, w):
                acc = acc + pbuf[b0 - k:b0 - k + seg, cols]
            parts.append(acc / cnt - cur)
        pooled = parts[0] if nseg == 1 else jnp.concatenate(parts, axis=0)
        yp = jnp.dot(pooled.astype(_BF16), poolw_ref[g], preferred_element_type=_F32)
        ycat[:, cols] = (yp * pscale_ref[:, cols]).astype(_BF16)
    if not sample:
        pbuf[0:POOL_HIST, :] = pbuf[CH:CH + POOL_HIST, :]

    zu = jnp.dot(h, win_ref[:, D_POOL:D_POOL + D_SGU], preferred_element_type=_F32)
    u = _gelu(zu)
    zv = jnp.dot(h, win_ref[:, D_POOL + D_SGU:D_POOL + 2 * D_SGU], preferred_element_type=_F32)
    v = _rms(_gelu(zv), sgug_ref[...])
    if sample:
        vout_ref[pl.ds(r0, CH), :] = v
    vb = v.astype(_BF16)
    row = lax.broadcasted_iota(jnp.int32, (CH, CH), 0)
    col = lax.broadcasted_iota(jnp.int32, (CH, CH), 1)
    keep = col <= row
    if sample:
        keep = keep & ((row // seg) == (col // seg))
    for hd in range(N_SGU_H):
        cols = slice(hd * GROUP, (hd + 1) * GROUP)
        wm = jnp.where(keep, sguw_ref[hd], 0.0).astype(_BF16)
        sg = jnp.dot(wm, vb[:, cols], preferred_element_type=_F32) + sgub_ref[:, cols]
        ycat[:, D_POOL + hd * GROUP:D_POOL + (hd + 1) * GROUP] = (u[:, cols] * sg).astype(_BF16)

    c0 = D_POOL + 2 * D_SGU
    za = jnp.dot(h, win_ref[:, c0:c0 + D_CONV], preferred_element_type=_F32)
    zg = jnp.dot(h, win_ref[:, c0 + D_CONV:c0 + 2 * D_CONV], preferred_element_type=_F32)
    c = za * jax.nn.sigmoid(zg)
    cout_ref[pl.ds(r0, CH), :] = c
    for s in range(nseg):
        if sample:
            cbuf[s * cstride:s * cstride + CONV_HIST, :] = sconv_ref[cc * nseg + s]
        cbuf[s * cstride + CONV_HIST:(s + 1) * cstride, :] = c[s * seg:(s + 1) * seg, :]
    first = CONV_HIST - (CONV_K - 1)
    for gq in range(D_CONV // GROUP):
        cols = slice(gq * GROUP, (gq + 1) * GROUP)
        for s in range(nseg):
            b0 = s * cstride + first
            acc = convw_ref[0:1, cols] * cbuf[b0:b0 + seg, cols]
            for k in range(1, CONV_K):
                acc = acc + convw_ref[k:k + 1, cols] * cbuf[b0 + k:b0 + k + seg, cols]
            ybuf[s * seg:(s + 1) * seg, cols] = acc + convb_ref[:, cols]
    if not sample:
        cbuf[0:CONV_HIST, :] = cbuf[CH:CH + CONV_HIST, :]
    y = ybuf[...]
    mu = jnp.mean(y, axis=-1, keepdims=True)
    yc = y - mu
    yn = yc * lax.rsqrt(jnp.mean(yc * yc, axis=-1, keepdims=True) + EPS) * lng_ref[...] + lnb_ref[...]
    ycat[:, D_POOL + D_SGU:] = (yn * jax.nn.sigmoid(yn)).astype(_BF16)

    m = jnp.dot(ycat[...], wout_ref[...], preferred_element_type=_F32)
    o_ref[pl.ds(r0, CH), :] = x + _rms(m, g3_ref[...])


def _mixer_call(x, spool, sconv, norm_g4, w_in, pool_w, pool_scale, sgu_g, sgu_w_p, sgu_w_s, sgu_b_p, sgu_b_s,
                conv_w, conv_b, ln_g, ln_b, w_out, layer):
    last = N_TILES - 1
    const = lambda *idx: (lambda i: idx)
    one = pl.Buffered(1)

    def kernel(x_ref, spool_ref, sconv_ref, g2_ref, g3_ref, win_ref, poolw_ref, pscale_ref, sgug_ref,
               sguwp_ref, sguws_ref, sgubp_ref, sgubs_ref, convw_ref, convb_ref, lng_ref, lnb_ref, wout_ref,
               o_ref, pout_ref, cout_ref, vout_ref, pbuf, cbuf, ybuf, ycat):
        tile = pl.program_id(0)
        common = (x_ref, spool_ref, sconv_ref, g2_ref, g3_ref, win_ref, poolw_ref, pscale_ref, sgug_ref)
        tail = (convw_ref, convb_ref, lng_ref, lnb_ref, wout_ref, o_ref, pout_ref, cout_ref, vout_ref,
                pbuf, cbuf, ybuf, ycat)

        @pl.when(tile == 0)
        def _():
            pbuf[0:POOL_HIST, :] = jnp.zeros((POOL_HIST, D_POOL), _F32)
            cbuf[0:CONV_HIST, :] = jnp.zeros((CONV_HIST, D_CONV), _F32)

        @pl.when(tile < last)
        def _():
            refs = common + (sguwp_ref, sgubp_ref) + tail

            def body(cc, carry):
                _mixer_chunk(cc, tile, CH, refs)
                return carry
            lax.fori_loop(0, TM // CH, body, 0)

        @pl.when(tile == last)
        def _():
            refs = common + (sguws_ref, sgubs_ref) + tail

            def body(cc, carry):
                _mixer_chunk(cc, tile, DEC_SEQ, refs)
                return carry
            lax.fori_loop(0, TM // CH, body, 0)

    n_sample_seg = CH // DEC_SEQ
    return pl.pallas_call(
        kernel,
        grid=(N_TILES,),
        in_specs=[
            pl.BlockSpec((TM, D_MODEL), lambda i: (i, 0)),
            pl.BlockSpec((None, DEC_BATCH, POOL_HIST, D_POOL), const(layer, 0, 0, 0)),
            pl.BlockSpec((None, DEC_BATCH, CONV_HIST, D_CONV), const(layer, 0, 0, 0)),
            pl.BlockSpec((None, None, 1, D_MODEL), const(layer, 2, 0, 0)),
            pl.BlockSpec((None, None, 1, D_MODEL), const(layer, 3, 0, 0)),
            pl.BlockSpec((None, D_MODEL, D_IN), const(layer, 0, 0), pipeline_mode=one),
            pl.BlockSpec((None, len(POOL_WINDOWS), GROUP, GROUP), const(layer, 0, 0, 0)),
            pl.BlockSpec((None, 1, D_POOL), const(layer, 0, 0)),
            pl.BlockSpec((None, 1, D_SGU), const(layer, 0, 0)),
            pl.BlockSpec((None, N_SGU_H, CH, CH), const(layer, 0, 0, 0)),
            pl.BlockSpec((None, N_SGU_H, CH, CH), const(layer, 0, 0, 0)),
            pl.BlockSpec((None, CH, D_SGU), const(layer, 0, 0)),
            pl.BlockSpec((None, CH, D_SGU), const(layer, 0, 0)),
            pl.BlockSpec((None, CONV_K, D_CONV), const(layer, 0, 0)),
            pl.BlockSpec((None, 1, D_CONV), const(layer, 0, 0)),
            pl.BlockSpec((None, 1, D_CONV), const(layer, 0, 0)),
            pl.BlockSpec((None, 1, D_CONV), const(layer, 0, 0)),
            pl.BlockSpec((None, D_MODEL, D_MODEL), const(layer, 0, 0), pipeline_mode=one),
        ],
        out_specs=[
            pl.BlockSpec((TM, D_MODEL), lambda i: (i, 0)),
            pl.BlockSpec((None, TM, D_POOL), lambda i: (i // last, 0, 0)),
            pl.BlockSpec((None, TM, D_CONV), lambda i: (i // last, 0, 0)),
            pl.BlockSpec((TM, D_SGU), lambda i: (0, 0)),
        ],
        out_shape=[
            jax.ShapeDtypeStruct((N_TOK, D_MODEL), _F32),
            jax.ShapeDtypeStruct((2, TM, D_POOL), _F32),
            jax.ShapeDtypeStruct((2, TM, D_CONV), _F32),
            jax.ShapeDtypeStruct((TM, D_SGU), _F32),
        ],
        scratch_shapes=[
            pltpu.VMEM((n_sample_seg * (POOL_HIST + DEC_SEQ), D_POOL), _F32),
            pltpu.VMEM((n_sample_seg * (CONV_HIST + DEC_SEQ), D_CONV), _F32),
            pltpu.VMEM((CH, D_CONV), _F32),
            pltpu.VMEM((CH, D_MODEL), _BF16),
        ],
        compiler_params=pltpu.CompilerParams(
            dimension_semantics=("arbitrary",), vmem_limit_bytes=VMEM_LIMIT),
        name=f"mixer_l{layer}",
    )(x, spool, sconv, norm_g4, norm_g4, w_in, pool_w, pool_scale, sgu_g, sgu_w_p, sgu_w_s, sgu_b_p, sgu_b_s,
      conv_w, conv_b, ln_g, ln_b, w_out)


def kernel(x_prompt, x_sample, state_pool, state_conv, norm_g, ffn_w_gate, ffn_w_up, ffn_w_down, w_in, pool_w,
           pool_scale, sgu_norm_g, sgu_w, sgu_b, conv_w, conv_b, conv_ln_g, conv_ln_b, w_out):
    x = jnp.concatenate([x_prompt.reshape(SEQ, D_MODEL), x_sample.reshape(DEC_BATCH * DEC_SEQ, D_MODEL)], axis=0)

    wg = ffn_w_gate.astype(_BF16)
    wu = ffn_w_up.astype(_BF16)
    wd = ffn_w_down.astype(_BF16)
    w_in_b = w_in.astype(_BF16)
    w_out_b = w_out.astype(_BF16)
    pool_w_b = pool_w.astype(_BF16)
    norm_g4 = norm_g.reshape(DEPTH, 6, 1, D_MODEL)
    rep = CH // DEC_SEQ
    sgu_w_s = jnp.tile(sgu_w[:, :, :DEC_SEQ, :DEC_SEQ], (1, 1, rep, rep))
    sgu_b_p = jnp.repeat(jnp.swapaxes(sgu_b, 1, 2), GROUP, axis=2)
    sgu_b_s = jnp.tile(sgu_b_p[:, :DEC_SEQ], (1, rep, 1))
    spool = jnp.pad(state_pool, ((0, 0), (0, 0), (POOL_HIST - (POOL_MAX - 1), 0), (0, 0)))
    sconv = jnp.pad(state_conv, ((0, 0), (0, 0), (CONV_HIST - (CONV_K - 1), 0), (0, 0)))
    r3 = lambda a: a.reshape(DEPTH, 1, a.shape[-1])

    pouts, couts, vouts = [], [], []
    for l in range(DEPTH):
        x = _ffn_call(x, norm_g4, wg, wu, wd, l, 0)
        x, pout, cout, vout = _mixer_call(
            x, spool, sconv, norm_g4, w_in_b, pool_w_b, r3(pool_scale), r3(sgu_norm_g), sgu_w, sgu_w_s,
            sgu_b_p, sgu_b_s, conv_w, r3(conv_b), r3(conv_ln_g), r3(conv_ln_b), w_out_b, l)
        x = _ffn_call(x, norm_g4, wg, wu, wd, l, 1)
        pouts.append(pout)
        couts.append(cout)
        vouts.append(vout)

    pout = jnp.stack(pouts)
    cout = jnp.stack(couts)
    vout = jnp.stack(vouts)
    n_pool, n_conv = POOL_MAX - 1, CONV_K - 1
    y_prompt = x[:SEQ].reshape(1, SEQ, D_MODEL)
    y_sample = x[SEQ:].reshape(DEC_BATCH, DEC_SEQ, D_MODEL)
    new_pool_prompt = pout[:, 0, TM - n_pool:, :].reshape(DEPTH, 1, n_pool, D_POOL)
    new_conv_prompt = cout[:, 0, TM - n_conv:, :].reshape(DEPTH, 1, n_conv, D_CONV)
    new_pool_sample = pout[:, 1].reshape(DEPTH, DEC_BATCH, DEC_SEQ, D_POOL)[:, :, DEC_SEQ - n_pool:, :]
    new_conv_sample = cout[:, 1].reshape(DEPTH, DEC_BATCH, DEC_SEQ, D_CONV)[:, :, DEC_SEQ - n_conv:, :]
    new_sgu_v_sample = vout.reshape(DEPTH, DEC_BATCH, DEC_SEQ, D_SGU)
    return (y_prompt, y_sample, new_pool_prompt, new_conv_prompt, new_pool_sample, new_conv_sample,
            new_sgu_v_sample)
```
